```python
import math
import jax, jax.numpy as jnp
from jax import lax
import numpy as np

D_MODEL = 4096
BATCH = 4
SEQ = 2048
DEPTH = 4
DEC_BATCH = 8
DEC_SEQ = 1
PAST_LEN = 8192
PAGE_SIZE = 128

N_MIXERS = 3
N_CONV_LAYERS = (DEPTH + 2) // 3
N_ATTN_LAYERS = (DEPTH + 1) // 3
N_POOL_LAYERS = DEPTH // 3
CONV_WIDTH = 3
N_HEADS = 32
HEAD_DIM = D_MODEL // N_HEADS
N_KV_HEADS = 8
GROUP = N_HEADS // N_KV_HEADS
IDX_HEADS = 32
IDX_DIM = 128
TOPK_MAX = 256
Q_BLOCK = 128
Q_DIM = N_HEADS * HEAD_DIM
KV_DIM = N_KV_HEADS * HEAD_DIM
IQ_DIM = IDX_HEADS * IDX_DIM
ATTN_IN_DIM = Q_DIM + 2 * KV_DIM + IQ_DIM + IDX_DIM + IDX_HEADS
ATTN_SPLITS = (Q_DIM, Q_DIM + KV_DIM, Q_DIM + 2 * KV_DIM, Q_DIM + 2 * KV_DIM + IQ_DIM,
               Q_DIM + 2 * KV_DIM + IQ_DIM + IDX_DIM)
REL_BUCKETS = 32
REL_MAX_EXACT = REL_BUCKETS // 2
REL_MAX_DIST = 128
POOL_WINDOWS = (2, 4, 8, 16)
POOL_GROUPS = len(POOL_WINDOWS)
POOL_GC = D_MODEL // POOL_GROUPS
POOL_PAD = max(POOL_WINDOWS) - 1
D_FF = 4 * D_MODEL
EPS = 1e-6

kernel_name = 'sparse_index_hybrid_step'


def rms_norm(x, g):
    xf = x.astype(jnp.float32)
    y = xf * lax.rsqrt(jnp.mean(xf * xf, axis=-1, keepdims=True) + EPS)
    return (y * g.astype(jnp.float32)).astype(x.dtype)


def sq_relu_mlp(h, w_up, w_down):
    a = jax.nn.relu(h @ w_up)
    return (a * a) @ w_down


def short_conv_mixer(h, prev, w_in, w_conv, w_out):
    T = h.shape[1]
    c, b, z = jnp.split(h @ w_in, 3, axis=-1)
    u = c * z
    ext = jnp.concatenate([prev.astype(u.dtype), u], axis=1)
    conv = w_conv[0] * ext[:, 0:T]
    for j in range(1, CONV_WIDTH):
        conv = conv + w_conv[j] * ext[:, j:j + T]
    y = (b * conv) @ w_out
    return y, ext[:, ext.shape[1] - (CONV_WIDTH - 1):]


def pool_mixer(h, prev, t0, w_pool, scale):
    B, T, D = h.shape
    ext = jnp.concatenate([prev.astype(h.dtype), h], axis=1)
    cs = jnp.cumsum(ext.astype(jnp.float32), axis=1)
    cs = jnp.concatenate([jnp.zeros((B, 1, D), jnp.float32), cs], axis=1)
    pos = t0 + jnp.arange(T)
    parts = []
    for g, w in enumerate(POOL_WINDOWS):
        lo, hi = g * POOL_GC, (g + 1) * POOL_GC
        s = cs[:, POOL_PAD + 1:POOL_PAD + 1 + T, lo:hi] - cs[:, POOL_PAD + 1 - w:POOL_PAD + 1 - w + T, lo:hi]
        cnt = jnp.minimum(pos + 1, w).astype(jnp.float32)
        parts.append(s / cnt[None, :, None])
    pooled = jnp.concatenate(parts, axis=-1)
    d = (pooled - h.astype(jnp.float32)).astype(h.dtype).reshape(B, T, POOL_GROUPS, POOL_GC)
    y = jnp.einsum('btgc,gcd->btgd', d, w_pool).reshape(B, T, D) * scale
    return y, ext[:, ext.shape[1] - POOL_PAD:]


def rel_bucket(dist):
    n = jnp.maximum(dist, 0)
    nf = jnp.maximum(n, 1).astype(jnp.float32)
    large = REL_MAX_EXACT + (jnp.log(nf / REL_MAX_EXACT) / math.log(REL_MAX_DIST / REL_MAX_EXACT)
                             * (REL_BUCKETS - REL_MAX_EXACT)).astype(jnp.int32)
    large = jnp.minimum(large, REL_BUCKETS - 1)
    return jnp.where(n < REL_MAX_EXACT, n, large)


def attn_project(h, w_in, q_g, k_g):
    B, T, _ = h.shape
    q, k, v, qi, ki, wi = jnp.split(h @ w_in, ATTN_SPLITS, axis=-1)
    q = rms_norm(q.reshape(B, T, N_HEADS, HEAD_DIM), q_g)
    k = rms_norm(k.reshape(B, T, N_KV_HEADS, HEAD_DIM), k_g)
    v = v.reshape(B, T, N_KV_HEADS, HEAD_DIM)
    qi = qi.reshape(B, T, IDX_HEADS, IDX_DIM)
    return q, k, v, qi, ki, wi


def index_select(qi, wi, ki, tpos, topk):
    L = ki.shape[1]
    dots = jnp.einsum('bqhd,bsd->bqhs', qi.astype(jnp.float32), ki.astype(jnp.float32)) * IDX_DIM ** -0.5
    score = jnp.einsum('bqh,bqhs->bqs', wi.astype(jnp.float32) * IDX_HEADS ** -0.5, jax.nn.relu(dots))
    spos = jnp.arange(L)
    score = jnp.where(spos[None, None, :] <= tpos[None, :, None], score, -jnp.inf)
    _, idx = lax.top_k(score, topk)
    return idx


def sparse_attend(q, k_sel, v_sel, idx, tpos, rel_bias):
    B, Q, _, _ = q.shape
    K = idx.shape[-1]
    qg = q.reshape(B, Q, N_KV_HEADS, GROUP, HEAD_DIM).astype(jnp.float32)
    logits = jnp.einsum('bqngd,bqjnd->bqngj', qg, k_sel.astype(jnp.float32)) * HEAD_DIM ** -0.5
    bias = rel_bias.astype(jnp.float32)[rel_bucket(tpos[None, :, None] - idx)]
    bias = bias.reshape(B, Q, K, N_KV_HEADS, GROUP).transpose(0, 1, 3, 4, 2)
    valid = (idx <= tpos[None, :, None])[:, :, None, None, :]
    logits = jnp.where(valid, logits + bias, -jnp.inf)
    p = jax.nn.softmax(logits, axis=-1)
    out = jnp.einsum('bqngj,bqjnd->bqngd', p, v_sel.astype(jnp.float32))
    return out.reshape(B, Q, Q_DIM).astype(q.dtype)


def dsa_prompt(h, w_in, q_g, k_g, w_o, rel_bias):
    B, S, _ = h.shape
    q, k, v, qi, ki, wi = attn_project(h, w_in, q_g, k_g)
    topk = min(TOPK_MAX, S // 4)
    nb = S // Q_BLOCK

    def blocks(a):
        return jnp.moveaxis(a.reshape((B, nb, Q_BLOCK) + a.shape[2:]), 1, 0)

    tpos = jnp.arange(S).reshape(nb, Q_BLOCK)
    take = jax.vmap(lambda a, i: a[i])

    def one_block(args):
        qb, qib, wib, tb = args
        idx = index_select(qib, wib, ki, tb, topk)
        k_sel = take(k, idx)
        v_sel = take(v, idx)
        return sparse_attend(qb, k_sel, v_sel, idx, tb, rel_bias)

    o = lax.map(one_block, (blocks(q), blocks(qi), blocks(wi), tpos))
    o = jnp.moveaxis(o, 0, 1).reshape(B, S, Q_DIM)
    return o @ w_o, k, v, ki


def dsa_sample(h, cache_k, cache_v, cache_ki, page_table, w_in, q_g, k_g, w_o, rel_bias):
    Bd, T, _ = h.shape
    q, k, v, qi, ki, wi = attn_project(h, w_in, q_g, k_g)
    past = page_table.shape[1] * PAGE_SIZE
    L = past + T
    topk = min(TOPK_MAX, L // 4)
    tpos = past + jnp.arange(T)
    ki_past = cache_ki[page_table].reshape(Bd, past, IDX_DIM)
    ki_all = jnp.concatenate([ki_past.astype(ki.dtype), ki], axis=1)
    idx = index_select(qi, wi, ki_all, tpos, topk)
    in_past = idx < past
    ip = jnp.minimum(idx, past - 1)
    phys = jnp.take_along_axis(page_table, (ip // PAGE_SIZE).reshape(Bd, -1), axis=1).reshape(idx.shape)
    slot = ip % PAGE_SIZE
    inew = jnp.clip(idx - past, 0, T - 1)
    take = jax.vmap(lambda a, i: a[i])
    k_sel = jnp.where(in_past[..., None, None], cache_k[phys, slot].astype(k.dtype), take(k, inew))
    v_sel = jnp.where(in_past[..., None, None], cache_v[phys, slot].astype(v.dtype), take(v, inew))
    o = sparse_attend(q, k_sel, v_sel, idx, tpos, rel_bias)
    return o @ w_o, k, v, ki


def setup_inputs(seed: int = 0) -> dict:
    key = jax.random.key(seed)
    ks = jax.random.split(key, 24)
    f32 = jnp.float32

    def nrm(k, shape, fan_in):
        return jax.random.normal(k, shape, f32) * fan_in ** -0.5

    def gain(k, shape):
        return 1.0 + 0.05 * jax.random.normal(k, shape, f32)

    n_pages = PAST_LEN // PAGE_SIZE
    n_used = DEC_BATCH * n_pages
    n_pool = n_used + max(1, n_used // 4)
    page_table = jax.random.permutation(ks[0], n_pool)[:n_used].reshape(DEC_BATCH, n_pages).astype(jnp.int32)
    return {
        'x_prompt': jax.random.normal(ks[1], (BATCH, SEQ, D_MODEL), f32),
        'x_sample': jax.random.normal(ks[2], (DEC_BATCH, DEC_SEQ, D_MODEL), f32),
        'cache_k': jax.random.normal(ks[3], (N_ATTN_LAYERS, n_pool, PAGE_SIZE, N_KV_HEADS, HEAD_DIM), f32),
        'cache_v': jax.random.normal(ks[4], (N_ATTN_LAYERS, n_pool, PAGE_SIZE, N_KV_HEADS, HEAD_DIM), f32),
        'cache_kidx': jax.random.normal(ks[5], (N_ATTN_LAYERS, n_pool, PAGE_SIZE, IDX_DIM), f32),
        'state_conv': jax.random.normal(ks[6], (N_CONV_LAYERS, DEC_BATCH, CONV_WIDTH - 1, D_MODEL), f32),
        'state_pool': jax.random.normal(ks[7], (N_POOL_LAYERS, DEC_BATCH, POOL_PAD, D_MODEL), f32),
        'page_table': page_table,
        'norm_mix_g': gain(ks[8], (DEPTH, D_MODEL)),
        'norm_mlp_g': gain(ks[9], (DEPTH, D_MODEL)),
        'conv_w_in': nrm(ks[10], (N_CONV_LAYERS, D_MODEL, 3 * D_MODEL), D_MODEL),
        'conv_w': nrm(ks[11], (N_CONV_LAYERS, CONV_WIDTH, D_MODEL), CONV_WIDTH),
        'conv_w_out': nrm(ks[12], (N_CONV_LAYERS, D_MODEL, D_MODEL), D_MODEL),
        'attn_w_in': nrm(ks[13], (N_ATTN_LAYERS, D_MODEL, ATTN_IN_DIM), D_MODEL),
        'attn_q_g': gain(ks[14], (N_ATTN_LAYERS, HEAD_DIM)),
        'attn_k_g': gain(ks[15], (N_ATTN_LAYERS, HEAD_DIM)),
        'attn_w_o': nrm(ks[16], (N_ATTN_LAYERS, Q_DIM, D_MODEL), Q_DIM),
        'rel_bias': 0.5 * jax.random.normal(ks[17], (REL_BUCKETS, N_HEADS), f32),
        'pool_w': nrm(ks[18], (N_POOL_LAYERS, POOL_GROUPS, POOL_GC, POOL_GC), POOL_GC),
        'pool_scale': gain(ks[19], (N_POOL_LAYERS, D_MODEL)),
        'mlp_w_up': nrm(ks[20], (DEPTH, D_MODEL, D_FF), D_MODEL),
        'mlp_w_down': nrm(ks[21], (DEPTH, D_FF, D_MODEL), D_FF),
    }


def reference(x_prompt, x_sample, cache_k, cache_v, cache_kidx, state_conv, state_pool, page_table,
              norm_mix_g, norm_mlp_g, conv_w_in, conv_w, conv_w_out, attn_w_in, attn_q_g, attn_k_g,
              attn_w_o, rel_bias, pool_w, pool_scale, mlp_w_up, mlp_w_down):
    xp, xs = x_prompt, x_sample
    Bp, S, D = xp.shape
    past = page_table.shape[1] * PAGE_SIZE
    kp, vp, kip, cvp, plp = [], [], [], [], []
    ksm, vsm, kis, cvs, pls = [], [], [], [], []
    for i in range(DEPTH):
        kind, j = i % N_MIXERS, i // N_MIXERS
        hp = rms_norm(xp, norm_mix_g[i])
        hs = rms_norm(xs, norm_mix_g[i])
        if kind == 0:
            yp, sp = short_conv_mixer(hp, jnp.zeros((Bp, CONV_WIDTH - 1, D), hp.dtype),
                                      conv_w_in[j], conv_w[j], conv_w_out[j])
            ys, ss = short_conv_mixer(hs, state_conv[j], conv_w_in[j], conv_w[j], conv_w_out[j])
            cvp.append(sp)
            cvs.append(ss)
        elif kind == 1:
            yp, k1, v1, ki1 = dsa_prompt(hp, attn_w_in[j], attn_q_g[j], attn_k_g[j], attn_w_o[j], rel_bias)
            ys, k2, v2, ki2 = dsa_sample(hs, cache_k[j], cache_v[j], cache_kidx[j], page_table,
                                         attn_w_in[j], attn_q_g[j], attn_k_g[j], attn_w_o[j], rel_bias)
            kp.append(k1); vp.append(v1); kip.append(ki1)
            ksm.append(k2); vsm.append(v2); kis.append(ki2)
        else:
            yp, sp = pool_mixer(hp, jnp.zeros((Bp, POOL_PAD, D), hp.dtype), 0, pool_w[j], pool_scale[j])
            ys, ss = pool_mixer(hs, state_pool[j], past, pool_w[j], pool_scale[j])
            plp.append(sp)
            pls.append(ss)
        xp = xp + yp
        xs = xs + ys
        xp = xp + sq_relu_mlp(rms_norm(xp, norm_mlp_g[i]), mlp_w_up[i], mlp_w_down[i])
        xs = xs + sq_relu_mlp(rms_norm(xs, norm_mlp_g[i]), mlp_w_up[i], mlp_w_down[i])
    return (xp, xs,
            jnp.stack(kp), jnp.stack(vp), jnp.stack(kip), jnp.stack(cvp), jnp.stack(plp),
            jnp.stack(ksm), jnp.stack(vsm), jnp.stack(kis), jnp.stack(cvs), jnp.stack(pls))
```

```python
import functools
import math

import numpy as np
import jax
import jax.numpy as jnp
from jax import lax
from jax.experimental import pallas as pl
from jax.experimental.pallas import tpu as pltpu

F32 = jnp.float32
BF16 = jnp.bfloat16

N_MIXERS = 3
CONV_WIDTH = 3
N_HEADS = 32
HEAD_DIM = 128
N_KV_HEADS = 8
GROUP = N_HEADS // N_KV_HEADS
IDX_HEADS = 32
IDX_DIM = 128
TOPK_MAX = 256
Q_BLOCK = 128
PAGE_SIZE = 128
REL_BUCKETS = 32
REL_MAX_EXACT = REL_BUCKETS // 2
REL_MAX_DIST = 128
POOL_WINDOWS = (2, 4, 8, 16)
POOL_PAD = max(POOL_WINDOWS) - 1
EPS = 1e-6

LANES = 128
V7X_VMEM_LIMIT_BYTES = 56 * 1024 * 1024
SAMPLE_ROWS = 16
NEG_MASK = -1e30
INT32_MIN = -(2 ** 31)


def _params(*semantics):
    return pltpu.CompilerParams(dimension_semantics=semantics,
                                vmem_limit_bytes=V7X_VMEM_LIMIT_BYTES)


def _rms_kernel(x_ref, g_ref, *o_refs, group):
    g = g_ref[...]
    for c in range(x_ref.shape[1] // group):
        x = x_ref[:, c * group:(c + 1) * group]
        y = x * lax.rsqrt(jnp.mean(x * x, axis=-1, keepdims=True) + EPS) * g
        for o in o_refs:
            o[:, c * group:(c + 1) * group] = y.astype(o.dtype)


def rms_norm(x, g, out_dtypes, *, cols=None, col_block=0):
    m = x.shape[0]
    cols = x.shape[1] if cols is None else cols
    group = g.shape[-1]
    tm = min(256, m)
    outs = pl.pallas_call(
        functools.partial(_rms_kernel, group=group),
        grid=(m // tm,),
        in_specs=[pl.BlockSpec((tm, cols), lambda i: (i, col_block)),
                  pl.BlockSpec((1, group), lambda i: (0, 0))],
        out_specs=[pl.BlockSpec((tm, cols), lambda i: (i, 0)) for _ in out_dtypes],
        out_shape=[jax.ShapeDtypeStruct((m, cols), dt) for dt in out_dtypes],
        compiler_params=_params("parallel"),
        name="rms_norm",
    )(x, g.reshape(1, group))
    return outs


def _gemm_kernel(a_ref, w_ref, *rest, nk, relu2, has_res):
    if has_res:
        r_ref, o_ref, *scratch = rest
    else:
        o_ref, *scratch = rest

    def finish(acc):
        if relu2:
            acc = jnp.maximum(acc, 0.0)
            acc = acc * acc
        if has_res:
            acc = r_ref[...] + acc
        o_ref[...] = acc.astype(o_ref.dtype)

    if nk == 1:
        finish(jnp.dot(a_ref[...], w_ref[...], preferred_element_type=F32))
    else:
        acc_ref, = scratch
        k = pl.program_id(2)

        @pl.when(k == 0)
        def _():
            acc_ref[...] = jnp.zeros_like(acc_ref)

        acc_ref[...] += jnp.dot(a_ref[...], w_ref[...], preferred_element_type=F32)

        @pl.when(k == nk - 1)
        def _():
            finish(acc_ref[...])


def gemm(a, w, *, out_dtype=F32, relu2=False, residual=None, tn_max=1024, tk_max=4096):
    m, kdim = a.shape
    n = w.shape[1]
    tm = min(1024, m)
    tn = min(tn_max, n)
    tk = min(tk_max, kdim)
    nk = kdim // tk
    in_specs = [pl.BlockSpec((tm, tk), lambda i, j, k: (i, k)),
                pl.BlockSpec((tk, tn), lambda i, j, k: (k, j))]
    args = [a, w]
    if residual is not None:
        in_specs.append(pl.BlockSpec((tm, tn), lambda i, j, k: (i, j)))
        args.append(residual)
    return pl.pallas_call(
        functools.partial(_gemm_kernel, nk=nk, relu2=relu2, has_res=residual is not None),
        grid=(m // tm, n // tn, nk),
        in_specs=in_specs,
        out_specs=pl.BlockSpec((tm, tn), lambda i, j, k: (i, j)),
        out_shape=jax.ShapeDtypeStruct((m, n), out_dtype),
        scratch_shapes=[pltpu.VMEM((tm, tn), F32)] if nk > 1 else [],
        compiler_params=_params("parallel", "parallel", "arbitrary"),
        name="gemm",
    )(*args)


def _conv_prompt_kernel(c_ref, b_ref, z_ref, w_ref, y_ref, st_ref):
    u = c_ref[...] * z_ref[...]
    t = u.shape[0]
    row = lax.broadcasted_iota(jnp.int32, u.shape, 0)
    u1 = jnp.where(row >= 1, pltpu.roll(u, 1, axis=0), 0.0)
    u2 = jnp.where(row >= 2, pltpu.roll(u, 2, axis=0), 0.0)
    conv = w_ref[0:1, :] * u2 + w_ref[1:2, :] * u1 + w_ref[2:3, :] * u
    y_ref[...] = (b_ref[...] * conv).astype(y_ref.dtype)
    st_ref[0] = c_ref[pl.ds(t - 2, 2), :] * z_ref[pl.ds(t - 2, 2), :]


def conv_prompt(cbz, w_conv, batch, seq):
    d = w_conv.shape[1]
    tc = min(256, d)
    nj = d // tc
    return pl.pallas_call(
        _conv_prompt_kernel,
        grid=(batch, nj),
        in_specs=[pl.BlockSpec((seq, tc), lambda b, j: (b, j)),
                  pl.BlockSpec((seq, tc), lambda b, j: (b, j + nj)),
                  pl.BlockSpec((seq, tc), lambda b, j: (b, j + 2 * nj)),
                  pl.BlockSpec((CONV_WIDTH, tc), lambda b, j: (0, j))],
        out_specs=[pl.BlockSpec((seq, tc), lambda b, j: (b, j)),
                   pl.BlockSpec((1, CONV_WIDTH - 1, tc), lambda b, j: (b, 0, j))],
        out_shape=[jax.ShapeDtypeStruct((batch * seq, d), BF16),
                   jax.ShapeDtypeStruct((batch, CONV_WIDTH - 1, d), F32)],
        compiler_params=_params("parallel", "parallel"),
        name="conv_prompt",
    )(cbz, cbz, cbz, w_conv)


def _conv_sample_kernel(c_ref, b_ref, z_ref, p0_ref, p1_ref, w_ref, y_ref, u_ref):
    u = c_ref[...] * z_ref[...]
    conv = w_ref[0:1, :] * p0_ref[...] + w_ref[1:2, :] * p1_ref[...] + w_ref[2:3, :] * u
    y_ref[...] = (b_ref[...] * conv).astype(y_ref.dtype)
    u_ref[...] = u


def conv_sample(cbz, prev0, prev1, w_conv):
    m = cbz.shape[0]
    d = w_conv.shape[1]
    tc = min(2048, d)
    nj = d // tc
    blk = lambda off: pl.BlockSpec((m, tc), lambda j: (0, j + off))
    return pl.pallas_call(
        _conv_sample_kernel,
        grid=(nj,),
        in_specs=[blk(0), blk(nj), blk(2 * nj), blk(0), blk(0),
                  pl.BlockSpec((CONV_WIDTH, tc), lambda j: (0, j))],
        out_specs=[blk(0), blk(0)],
        out_shape=[jax.ShapeDtypeStruct((m, d), BF16), jax.ShapeDtypeStruct((m, d), F32)],
        compiler_params=_params("parallel"),
        name="conv_sample",
    )(cbz, cbz, cbz, prev0, prev1, w_conv)


POOL_HALO = 16


def _pool_prompt_kernel(h_ref, halo_ref, x_ref, w_ref, s_ref, o_ref):
    g = pl.program_id(0)
    i = pl.program_id(2)
    main = h_ref[0]
    tm = main.shape[0]
    halo = jnp.where(i > 0, halo_ref[0], 0.0)
    ext = jnp.concatenate([halo, main], axis=0)
    s2 = ext + pltpu.roll(ext, 1, axis=0)
    s4 = s2 + pltpu.roll(s2, 2, axis=0)
    s8 = s4 + pltpu.roll(s4, 4, axis=0)
    s16 = s8 + pltpu.roll(s8, 8, axis=0)
    win = jnp.where(g == 0, s2, jnp.where(g == 1, s4, jnp.where(g == 2, s8, s16)))[POOL_HALO:]
    width = lax.shift_left(jnp.int32(2), g)
    pos = i * tm + lax.broadcasted_iota(jnp.int32, (tm, 1), 0)
    cnt = jnp.minimum(pos + 1, width).astype(F32)
    d = (win / cnt - main).astype(BF16)
    y = jnp.dot(d, w_ref[0], preferred_element_type=F32) * s_ref[...]
    o_ref[0] = x_ref[0] + y


def pool_prompt(h, x, w_pool, scale, batch, seq):
    d = h.shape[1]
    ng, gc = w_pool.shape[0], w_pool.shape[1]
    tm = min(256, seq)
    ratio = tm // POOL_HALO
    h3 = h.reshape(batch, seq, d)
    x3 = x.reshape(batch, seq, d)
    out = pl.pallas_call(
        _pool_prompt_kernel,
        grid=(ng, batch, seq // tm),
        in_specs=[pl.BlockSpec((1, tm, gc), lambda g, b, i: (b, i, g)),
                  pl.BlockSpec((1, POOL_HALO, gc), lambda g, b, i: (b, jnp.maximum(i * ratio - 1, 0), g)),
                  pl.BlockSpec((1, tm, gc), lambda g, b, i: (b, i, g)),
                  pl.BlockSpec((1, gc, gc), lambda g, b, i: (g, 0, 0)),
                  pl.BlockSpec((1, gc), lambda g, b, i: (0, g))],
        out_specs=pl.BlockSpec((1, tm, gc), lambda g, b, i: (b, i, g)),
        out_shape=jax.ShapeDtypeStruct((batch, seq, d), F32),
        compiler_params=_params("parallel", "parallel", "parallel"),
        name="pool_prompt",
    )(h3, h3, x3, w_pool, scale.reshape(1, d))
    return out.reshape(batch * seq, d)


def _pool_sample_kernel(h_ref, st_ref, x_ref, w_ref, s_ref, o_ref, *, pos):
    g = pl.program_id(0)
    h = h_ref[...]
    sums = []
    acc = h
    used = 0
    for w in POOL_WINDOWS:
        for r in range(used, w - 1):
            acc = acc + st_ref[POOL_PAD - 1 - r]
        used = w - 1
        sums.append(acc / float(min(pos + 1, w)))
    pooled = jnp.where(g == 0, sums[0], jnp.where(g == 1, sums[1], jnp.where(g == 2, sums[2], sums[3])))
    d = (pooled - h).astype(BF16)
    y = jnp.dot(d, w_ref[0], preferred_element_type=F32) * s_ref[...]
    o_ref[...] = x_ref[...] + y


def pool_sample(h, state_t, x, w_pool, scale, pos):
    m, d = h.shape
    ng, gc = w_pool.shape[0], w_pool.shape[1]
    return pl.pallas_call(
        functools.partial(_pool_sample_kernel, pos=pos),
        grid=(ng,),
        in_specs=[pl.BlockSpec((m, gc), lambda g: (0, g)),
                  pl.BlockSpec((POOL_PAD, m, gc), lambda g: (0, 0, g)),
                  pl.BlockSpec((m, gc), lambda g: (0, g)),
                  pl.BlockSpec((1, gc, gc), lambda g: (g, 0, 0)),
                  pl.BlockSpec((1, gc), lambda g: (0, g))],
        out_specs=pl.BlockSpec((m, gc), lambda g: (0, g)),
        out_shape=jax.ShapeDtypeStruct((m, d), F32),
        compiler_params=_params("parallel"),
        name="pool_sample",
    )(h, state_t, x, w_pool, scale.reshape(1, d))


def _count(pred):
    return jnp.sum(jnp.where(pred, 1.0, 0.0), axis=1, keepdims=True)


def _topk_additive_mask(score, valid, topk):
    rows, cols = score.shape
    kf = float(topk)
    score = jnp.where(valid, score + 0.0, -jnp.inf)
    bits = lax.bitcast_convert_type(score, jnp.int32)
    key = bits ^ (lax.shift_right_arithmetic(bits, 31) & 0x7FFFFFFF)

    thr = jnp.where(_count(key >= 0) >= kf, 0, INT32_MIN).astype(jnp.int32)

    def value_step(i, thr):
        cand = thr | lax.shift_left(jnp.int32(1), 30 - i)
        return jnp.where(_count(key >= cand) >= kf, cand, thr)

    thr = lax.fori_loop(0, 31, value_step, thr)

    above = key > thr
    tied = key == thr
    room = kf - _count(above)
    col = lax.broadcasted_iota(jnp.int32, (rows, cols), 1)
    nbits = cols.bit_length()

    def column_step(i, bound):
        cand = bound | lax.shift_left(jnp.int32(1), nbits - 1 - i)
        return jnp.where(_count(tied & (col < cand)) <= room, cand, bound)

    bound = lax.fori_loop(0, nbits, column_step, jnp.zeros((rows, 1), jnp.int32))
    chosen = (above | (tied & (col < bound))) & valid
    return jnp.where(chosen, 0.0, NEG_MASK)


def _index_mask_kernel(qi_ref, wi_ref, kit_ref, o_ref, *, topk):
    qb = pl.program_id(1)
    kit = kit_ref[0]
    seq = kit.shape[1]
    wi = wi_ref[...]
    score = jnp.zeros((Q_BLOCK, seq), F32)
    for h in range(IDX_HEADS):
        dots = jnp.dot(qi_ref[0, :, h * IDX_DIM:(h + 1) * IDX_DIM], kit, preferred_element_type=F32)
        score = score + wi[:, h:h + 1] * jnp.maximum(dots, 0.0)
    score = score * (IDX_DIM ** -0.5 * IDX_HEADS ** -0.5)
    tpos = qb * Q_BLOCK + lax.broadcasted_iota(jnp.int32, (Q_BLOCK, 1), 0)
    spos = lax.broadcasted_iota(jnp.int32, (Q_BLOCK, seq), 1)
    o_ref[0] = _topk_additive_mask(score, spos <= tpos, topk)


def index_mask_prompt(qi, kw, ki_t, batch, seq, topk):
    nb = seq // Q_BLOCK
    return pl.pallas_call(
        functools.partial(_index_mask_kernel, topk=topk),
        grid=(batch, nb),
        in_specs=[pl.BlockSpec((1, Q_BLOCK, IDX_HEADS * IDX_DIM), lambda b, q: (b, q, 0)),
                  pl.BlockSpec((Q_BLOCK, LANES), lambda b, q: (b * nb + q, 1)),
                  pl.BlockSpec((1, IDX_DIM, seq), lambda b, q: (b, 0, 0))],
        out_specs=pl.BlockSpec((1, Q_BLOCK, seq), lambda b, q: (b, q, 0)),
        out_shape=jax.ShapeDtypeStruct((batch, seq, seq), F32),
        compiler_params=_params("parallel", "parallel"),
        name="index_mask_prompt",
    )(qi.reshape(batch, seq, IDX_HEADS * IDX_DIM), kw, ki_t)


def _rel_bucket_np(dist):
    n = np.maximum(dist, 0)
    nf = np.maximum(n, 1).astype(np.float32)
    large = REL_MAX_EXACT + (np.log(nf / REL_MAX_EXACT) / math.log(REL_MAX_DIST / REL_MAX_EXACT)
                             * (REL_BUCKETS - REL_MAX_EXACT)).astype(np.int32)
    large = np.minimum(large, REL_BUCKETS - 1)
    return np.where(n < REL_MAX_EXACT, n, large).astype(np.int32)


def _bias_table_kernel(relb_ref, bkt_ref, o_ref):
    h = pl.program_id(0)
    for j in range(3):
        bkt = bkt_ref[j]
        tab = jnp.zeros(bkt.shape, F32)
        for k in range(REL_BUCKETS):
            tab = jnp.where(bkt == k, relb_ref[k, h], tab)
        o_ref[0, j] = tab


def prompt_bias_table(rel_bias):
    r = np.arange(Q_BLOCK)[:, None]
    c = np.arange(Q_BLOCK)[None, :]
    assert np.all(_rel_bucket_np(np.arange(Q_BLOCK + 1, 1024 * Q_BLOCK)) == REL_BUCKETS - 1)
    bkt = np.stack([_rel_bucket_np(r - c), _rel_bucket_np(Q_BLOCK + r - c),
                    _rel_bucket_np(2 * Q_BLOCK + r - c)])
    return pl.pallas_call(
        _bias_table_kernel,
        grid=(N_HEADS,),
        in_specs=[pl.BlockSpec(memory_space=pltpu.SMEM),
                  pl.BlockSpec((3, Q_BLOCK, Q_BLOCK), lambda h: (0, 0, 0))],
        out_specs=pl.BlockSpec((1, 3, Q_BLOCK, Q_BLOCK), lambda h: (h, 0, 0, 0)),
        out_shape=jax.ShapeDtypeStruct((N_HEADS, 3, Q_BLOCK, Q_BLOCK), F32),
        compiler_params=_params("parallel"),
        name="bias_table",
    )(rel_bias, jnp.asarray(bkt))


def _attn_prompt_kernel(q_ref, k_ref, v_ref, m_ref, bias_ref, o_ref):
    qb = pl.program_id(1)
    scale = HEAD_DIM ** -0.5
    rows = GROUP * Q_BLOCK
    for n in range(N_KV_HEADS):
        lanes = slice(n * HEAD_DIM, (n + 1) * HEAD_DIM)
        qg = jnp.concatenate(
            [q_ref[0, :, (n * GROUP + g) * HEAD_DIM:(n * GROUP + g + 1) * HEAD_DIM] for g in range(GROUP)],
            axis=0)

        def body(kb, carry, n=n, lanes=lanes, qg=qg):
            m_run, l_run, acc = carry
            ks = pl.multiple_of(kb * Q_BLOCK, Q_BLOCK)
            kc = k_ref[0, pl.ds(ks, Q_BLOCK), lanes]
            vc = v_ref[0, pl.ds(ks, Q_BLOCK), lanes]
            s = lax.dot_general(qg, kc, (((1,), (1,)), ((), ())), preferred_element_type=F32) * scale
            j = jnp.minimum(qb - kb, 2)
            bias = jnp.concatenate([bias_ref[n * GROUP + g, j] for g in range(GROUP)], axis=0)
            madd = m_ref[0, :, pl.ds(ks, Q_BLOCK)]
            s = s + bias + jnp.concatenate([madd] * GROUP, axis=0)
            m_new = jnp.maximum(m_run, jnp.max(s, axis=1, keepdims=True))
            alpha = jnp.exp(m_run - m_new)
            p = jnp.exp(s - m_new)
            l_new = alpha * l_run + jnp.sum(p, axis=1, keepdims=True)
            acc = alpha * acc + jnp.dot(p.astype(BF16), vc, preferred_element_type=F32)
            return m_new, l_new, acc

        init = (jnp.full((rows, 1), NEG_MASK, F32), jnp.zeros((rows, 1), F32),
                jnp.zeros((rows, HEAD_DIM), F32))
        _, l_fin, acc = lax.fori_loop(0, qb + 1, body, init)
        out = acc / l_fin
        for g in range(GROUP):
            hl = slice((n * GROUP + g) * HEAD_DIM, (n * GROUP + g + 1) * HEAD_DIM)
            o_ref[0, :, hl] = out[g * Q_BLOCK:(g + 1) * Q_BLOCK].astype(o_ref.dtype)


def attn_prompt(q, k, v, madd, bias, batch, seq):
    nb = seq // Q_BLOCK
    qd = N_HEADS * HEAD_DIM
    kvd = N_KV_HEADS * HEAD_DIM
    out = pl.pallas_call(
        _attn_prompt_kernel,
        grid=(batch, nb),
        in_specs=[pl.BlockSpec((1, Q_BLOCK, qd), lambda b, i: (b, i, 0)),
                  pl.BlockSpec((1, seq, kvd), lambda b, i: (b, 0, 0)),
                  pl.BlockSpec((1, seq, kvd), lambda b, i: (b, 0, 0)),
                  pl.BlockSpec((1, Q_BLOCK, seq), lambda b, i: (b, i, 0)),
                  pl.BlockSpec((N_HEADS, 3, Q_BLOCK, Q_BLOCK), lambda b, i: (0, 0, 0, 0))],
        out_specs=pl.BlockSpec((1, Q_BLOCK, qd), lambda b, i: (b, i, 0)),
        out_shape=jax.ShapeDtypeStruct((batch, seq, qd), BF16),
        compiler_params=_params("parallel", "parallel"),
        name="attn_prompt",
    )(q.reshape(batch, seq, qd), k.reshape(batch, seq, kvd), v.reshape(batch, seq, kvd), madd, bias)
    return out.reshape(batch * seq, qd)


def _index_score_sample_kernel(pt_ref, qi_ref, wi_ref, ki_ref, o_ref):
    del pt_ref
    dots = lax.dot_general(qi_ref[0], ki_ref[0].astype(BF16), (((1,), (1,)), ((), ())),
                           preferred_element_type=F32)
    sc = jnp.sum(wi_ref[0] * jnp.maximum(dots, 0.0), axis=0, keepdims=True)
    o_ref[0, 0] = sc * (IDX_DIM ** -0.5 * IDX_HEADS ** -0.5)


def index_score_sample(page_table, qi3, wi3, cache_ki):
    nbatch, npages = page_table.shape
    grid_spec = pltpu.PrefetchScalarGridSpec(
        num_scalar_prefetch=1,
        grid=(nbatch, npages),
        in_specs=[pl.BlockSpec((1, IDX_HEADS, IDX_DIM), lambda b, p, pt: (b, 0, 0)),
                  pl.BlockSpec((1, IDX_HEADS, 1), lambda b, p, pt: (b, 0, 0)),
                  pl.BlockSpec((1, PAGE_SIZE, IDX_DIM), lambda b, p, pt: (pt[b, p], 0, 0))],
        out_specs=pl.BlockSpec((1, 1, 1, PAGE_SIZE), lambda b, p, pt: (b, p, 0, 0)),
    )
    return pl.pallas_call(
        _index_score_sample_kernel,
        grid_spec=grid_spec,
        out_shape=jax.ShapeDtypeStruct((nbatch, npages, 1, PAGE_SIZE), F32),
        compiler_params=_params("arbitrary", "arbitrary"),
        name="index_score_sample",
    )(page_table, qi3, wi3, cache_ki)


def _index_mask_sample_kernel(sc_ref, qi_ref, wi_ref, kin_ref, o_ref, *, past, topk):
    dn = jnp.sum(qi_ref[0].astype(F32) * kin_ref[0], axis=1, keepdims=True)
    sn = jnp.sum(wi_ref[0] * jnp.maximum(dn, 0.0), axis=0, keepdims=True)
    sn = sn * (IDX_DIM ** -0.5 * IDX_HEADS ** -0.5)
    lane = lax.broadcasted_iota(jnp.int32, (1, PAGE_SIZE), 1)
    tail = jnp.where(lane == 0, sn, 0.0)
    score = jnp.concatenate([sc_ref[0], tail], axis=1)
    col = lax.broadcasted_iota(jnp.int32, score.shape, 1)
    o_ref[0] = _topk_additive_mask(score, col <= past, topk)


def index_mask_sample(scores, qi3, wi3, ki_new, past, topk):
    nbatch = scores.shape[0]
    width = past + PAGE_SIZE
    return pl.pallas_call(
        functools.partial(_index_mask_sample_kernel, past=past, topk=topk),
        grid=(nbatch,),
        in_specs=[pl.BlockSpec((1, 1, past), lambda b: (b, 0, 0)),
                  pl.BlockSpec((1, IDX_HEADS, IDX_DIM), lambda b: (b, 0, 0)),
                  pl.BlockSpec((1, IDX_HEADS, 1), lambda b: (b, 0, 0)),
                  pl.BlockSpec((1, 1, IDX_DIM), lambda b: (b, 0, 0))],
        out_specs=pl.BlockSpec((1, 1, width), lambda b: (b, 0, 0)),
        out_shape=jax.ShapeDtypeStruct((nbatch, 1, width), F32),
        compiler_params=_params("parallel"),
        name="index_mask_sample",
    )(scores.reshape(nbatch, 1, past), qi3, wi3, ki_new)


def _attn_sample_kernel(pt_ref, q_ref, k_ref, v_ref, m_ref, mn_ref, kn_ref, vn_ref, relb_ref, bkt_ref,
                        o_ref, m_scr, l_scr, acc_scr):
    del pt_ref
    p = pl.program_id(1)
    npages = pl.num_programs(1)
    scale = HEAD_DIM ** -0.5

    @pl.when(p == 0)
    def _():
        m_scr[...] = jnp.full_like(m_scr, NEG_MASK)
        l_scr[...] = jnp.zeros_like(l_scr)
        acc_scr[...] = jnp.zeros_like(acc_scr)

    relb = relb_ref[...]
    bkt = jnp.where(p == npages - 1, bkt_ref[...], REL_BUCKETS - 1)
    bias = jnp.zeros((N_HEADS, PAGE_SIZE), F32)
    for kk in range(REL_BUCKETS):
        bias = jnp.where(bkt == kk, relb[:, kk:kk + 1], bias)

    q = q_ref[0]
    s = lax.dot_general(q, k_ref[0].astype(BF16), (((1,), (1,)), ((), ())),
                        preferred_element_type=F32) * scale
    s = s + bias + m_ref[0, 0]
    m_new = jnp.maximum(m_scr[...], jnp.max(s, axis=1, keepdims=True))
    alpha = jnp.exp(m_scr[...] - m_new)
    pr = jnp.exp(s - m_new)
    l_scr[...] = alpha * l_scr[...] + jnp.sum(pr, axis=1, keepdims=True)
    acc_scr[...] = alpha * acc_scr[...] + jnp.dot(pr.astype(BF16), v_ref[0].astype(BF16),
                                                   preferred_element_type=F32)
    m_scr[...] = m_new

    @pl.when(p == npages - 1)
    def _():
        sn = jnp.sum(q.astype(F32) * kn_ref[0], axis=1, keepdims=True) * scale
        sn = sn + relb[:, 0:1] + mn_ref[0, 0][:, 0:1]
        m_fin = jnp.maximum(m_scr[...], sn)
        a = jnp.exp(m_scr[...] - m_fin)
        pn = jnp.exp(sn - m_fin)
        l_fin = a * l_scr[...] + pn
        acc = (a * acc_scr[...] + pn * vn_ref[0]) / l_fin
        head = lax.broadcasted_iota(jnp.int32, (N_HEADS, HEAD_DIM), 0)
        out = jnp.zeros((N_HEADS, HEAD_DIM), F32)
        for n in range(N_KV_HEADS):
            out = jnp.where(head // GROUP == n, acc[:, n * HEAD_DIM:(n + 1) * HEAD_DIM], out)
        o_ref[0] = out


def attn_sample(page_table, q_rows, cache_k, cache_v, madd, k_new, v_new, relb_t, bkt_last):
    nbatch, npages = page_table.shape
    kvd = N_KV_HEADS * HEAD_DIM
    grid_spec = pltpu.PrefetchScalarGridSpec(
        num_scalar_prefetch=1,
        grid=(nbatch, npages),
        in_specs=[pl.BlockSpec((1, N_HEADS, kvd), lambda b, p, pt: (b, 0, 0)),
                  pl.BlockSpec((1, PAGE_SIZE, kvd), lambda b, p, pt: (pt[b, p], 0, 0)),
                  pl.BlockSpec((1, PAGE_SIZE, kvd), lambda b, p, pt: (pt[b, p], 0, 0)),
                  pl.BlockSpec((1, 1, 1, PAGE_SIZE), lambda b, p, pt: (b, p, 0, 0)),
                  pl.BlockSpec((1, 1, 1, PAGE_SIZE), lambda b, p, pt: (b, npages, 0, 0)),
                  pl.BlockSpec((1, 1, kvd), lambda b, p, pt: (b, 0, 0)),
                  pl.BlockSpec((1, 1, kvd), lambda b, p, pt: (b, 0, 0)),
                  pl.BlockSpec((N_HEADS, LANES), lambda b, p, pt: (0, 0)),
                  pl.BlockSpec((1, PAGE_SIZE), lambda b, p, pt: (0, 0))],
        out_specs=pl.BlockSpec((1, N_HEADS, HEAD_DIM), lambda b, p, pt: (b, 0, 0)),
        scratch_shapes=[pltpu.VMEM((N_HEADS, 1), F32), pltpu.VMEM((N_HEADS, 1), F32),
                        pltpu.VMEM((N_HEADS, kvd), F32)],
    )
    return pl.pallas_call(
        _attn_sample_kernel,
        grid_spec=grid_spec,
        out_shape=jax.ShapeDtypeStruct((nbatch, N_HEADS, HEAD_DIM), F32),
        compiler_params=_params("arbitrary", "arbitrary"),
        name="attn_sample",
    )(page_table, q_rows, cache_k, cache_v, madd, madd, k_new, v_new, relb_t, bkt_last)


def _pad_rows(a, rows):
    return jnp.pad(a, ((0, rows - a.shape[0]),) + ((0, 0),) * (a.ndim - 1))


def _mlp(x, g, w_up, w_down):
    h, = rms_norm(x, g, [BF16])
    a = gemm(h, w_up, out_dtype=BF16, relu2=True)
    return gemm(a, w_down, residual=x, tk_max=2048)


def _conv_layer(xp, xs, g, w_in, w_conv, w_out, state, batch, seq, nsample):
    hp, = rms_norm(xp, g, [BF16])
    hs, = rms_norm(xs, g, [BF16])
    yp_in, st_p = conv_prompt(gemm(hp, w_in), w_conv, batch, seq)
    prev = _pad_rows(state, SAMPLE_ROWS)
    ys_in, u_s = conv_sample(gemm(hs, w_in), prev[:, 0], prev[:, 1], w_conv)
    st_s = jnp.concatenate([state[:, 1:], u_s[:nsample, None, :]], axis=1)
    xp = gemm(yp_in, w_out, residual=xp)
    xs = gemm(ys_in, w_out, residual=xs)
    return xp, xs, st_p, st_s


def _pool_layer(xp, xs, g, w_pool, scale, state, batch, seq, nsample, past):
    d = xp.shape[1]
    hp, = rms_norm(xp, g, [F32])
    hs, = rms_norm(xs, g, [F32])
    st_p = hp.reshape(batch, seq, d)[:, seq - POOL_PAD:]
    st_s = jnp.concatenate([state[:, 1:], hs[:nsample, None, :]], axis=1)
    xp = pool_prompt(hp, xp, w_pool, scale, batch, seq)
    state_t = jnp.transpose(_pad_rows(state, SAMPLE_ROWS), (1, 0, 2))
    xs = pool_sample(hs, state_t, xs, w_pool, scale, past)
    return xp, xs, st_p, st_s


def _attn_project(x, g, w_q, w_kv, w_qi, w_kw, q_g, k_g):
    kvd = N_KV_HEADS * HEAD_DIM
    h, = rms_norm(x, g, [BF16])
    q, = rms_norm(gemm(h, w_q), q_g, [BF16])
    kv = gemm(h, w_kv)
    k_f32, k_bf = rms_norm(kv, k_g, [F32, BF16], cols=kvd, col_block=0)
    v_f32 = kv[:, kvd:]
    qi = gemm(h, w_qi, out_dtype=BF16)
    kw = gemm(h, w_kw, tn_max=2 * LANES)
    return q, k_f32, k_bf, v_f32, qi, kw


def _attn_layer(xp, xs, g, w_in, q_g, k_g, w_o, rel_bias, cache_k, cache_v, cache_ki, page_table,
                batch, seq, nsample):
    qd = N_HEADS * HEAD_DIM
    kvd = N_KV_HEADS * HEAD_DIM
    iqd = IDX_HEADS * IDX_DIM
    w_q = w_in[:, :qd].astype(BF16)
    w_kv = w_in[:, qd:qd + 2 * kvd].astype(BF16)
    w_qi = w_in[:, qd + 2 * kvd:qd + 2 * kvd + iqd].astype(BF16)
    w_kw = jnp.pad(w_in[:, qd + 2 * kvd + iqd:], ((0, 0), (0, 2 * LANES - IDX_DIM - IDX_HEADS))).astype(BF16)
    w_o = w_o.astype(BF16)

    q, k_f32, k_bf, v_f32, qi, kw = _attn_project(xp, g, w_q, w_kv, w_qi, w_kw, q_g, k_g)
    ki_f32 = kw[:, :IDX_DIM]
    ki_t = jnp.transpose(ki_f32.astype(BF16).reshape(batch, seq, IDX_DIM), (0, 2, 1))
    madd = index_mask_prompt(qi, kw, ki_t, batch, seq, min(TOPK_MAX, seq // 4))
    bias = prompt_bias_table(rel_bias)
    o = attn_prompt(q, k_bf, v_f32.astype(BF16), madd, bias, batch, seq)
    xp = gemm(o, w_o, residual=xp)
    new_p = (k_f32.reshape(batch, seq, N_KV_HEADS, HEAD_DIM), v_f32.reshape(batch, seq, N_KV_HEADS, HEAD_DIM),
             ki_f32.reshape(batch, seq, IDX_DIM))

    npages = page_table.shape[1]
    past = npages * PAGE_SIZE
    topk = min(TOPK_MAX, (past + 1) // 4)
    q, k_f32, _, v_f32, qi, kw = _attn_project(xs, g, w_q, w_kv, w_qi, w_kw, q_g, k_g)
    k_new = k_f32[:nsample].reshape(nsample, 1, kvd)
    v_new = v_f32[:nsample].reshape(nsample, 1, kvd)
    ki_new = kw[:nsample, :IDX_DIM].reshape(nsample, 1, IDX_DIM)
    qi3 = qi[:nsample].reshape(nsample, IDX_HEADS, IDX_DIM)
    wi3 = kw[:nsample, IDX_DIM:IDX_DIM + IDX_HEADS].reshape(nsample, IDX_HEADS, 1)
    scores = index_score_sample(page_table, qi3, wi3, cache_ki)
    madd = index_mask_sample(scores, qi3, wi3, ki_new, past, topk)
    madd = madd.reshape(nsample, npages + 1, 1, PAGE_SIZE)
    own_kv_head = jnp.asarray(np.arange(N_HEADS)[:, None] // GROUP == np.arange(N_KV_HEADS)[None, :])
    q3 = q[:nsample].reshape(nsample, N_HEADS, 1, HEAD_DIM)
    q_rows = jnp.where(own_kv_head[None, :, :, None], q3, jnp.zeros((), q.dtype)).reshape(nsample, N_HEADS, kvd)
    relb_t = jnp.pad(rel_bias.T, ((0, 0), (0, LANES - REL_BUCKETS)))
    bkt_last = jnp.asarray(_rel_bucket_np(PAGE_SIZE - np.arange(PAGE_SIZE))[None, :])
    o = attn_sample(page_table, q_rows, cache_k.reshape(-1, PAGE_SIZE, kvd), cache_v.reshape(-1, PAGE_SIZE, kvd),
                    madd, k_new, v_new, relb_t, bkt_last)
    o = _pad_rows(o.reshape(nsample, qd), SAMPLE_ROWS).astype(BF16)
    xs = gemm(o, w_o, residual=xs)
    new_s = (k_new.reshape(nsample, 1, N_KV_HEADS, HEAD_DIM), v_new.reshape(nsample, 1, N_KV_HEADS, HEAD_DIM),
             ki_new)
    return xp, xs, new_p, new_s


def kernel(x_prompt, x_sample, cache_k, cache_v, cache_kidx, state_conv, state_pool, page_table, norm_mix_g, norm_mlp_g, conv_w_in, conv_w, conv_w_out, attn_w_in, attn_q_g, attn_k_g, attn_w_o, rel_bias, pool_w, pool_scale, mlp_w_up, mlp_w_down):
    batch, seq, d = x_prompt.shape
    nsample, dec_seq, _ = x_sample.shape
    assert dec_seq == 1 and nsample <= SAMPLE_ROWS and seq % Q_BLOCK == 0
    depth = norm_mix_g.shape[0]
    past = page_table.shape[1] * PAGE_SIZE

    xp = x_prompt.reshape(batch * seq, d)
    xs = _pad_rows(x_sample.reshape(nsample, d), SAMPLE_ROWS)
    kp, vp, kip, cvp, plp = [], [], [], [], []
    ksm, vsm, kis, cvs, pls = [], [], [], [], []
    for i in range(depth):
        kind, j = i % N_MIXERS, i // N_MIXERS
        if kind == 0:
            xp, xs, st_p, st_s = _conv_layer(xp, xs, norm_mix_g[i], conv_w_in[j].astype(BF16), conv_w[j],
                                             conv_w_out[j].astype(BF16), state_conv[j], batch, seq, nsample)
            cvp.append(st_p)
            cvs.append(st_s)
        elif kind == 1:
            xp, xs, new_p, new_s = _attn_layer(xp, xs, norm_mix_g[i], attn_w_in[j], attn_q_g[j], attn_k_g[j],
                                               attn_w_o[j], rel_bias, cache_k[j], cache_v[j], cache_kidx[j],
                                               page_table, batch, seq, nsample)
            kp.append(new_p[0]); vp.append(new_p[1]); kip.append(new_p[2])
            ksm.append(new_s[0]); vsm.append(new_s[1]); kis.append(new_s[2])
        else:
            xp, xs, st_p, st_s = _pool_layer(xp, xs, norm_mix_g[i], pool_w[j].astype(BF16), pool_scale[j],
                                             state_pool[j], batch, seq, nsample, past)
            plp.append(st_p)
            pls.append(st_s)
        w_up = mlp_w_up[i].astype(BF16)
        w_down = mlp_w_down[i].astype(BF16)
        xp = _mlp(xp, norm_mlp_g[i], w_up, w_down)
        xs = _mlp(xs, norm_mlp_g[i], w_up, w_down)
    return (xp.reshape(batch, seq, d), xs[:nsample].reshape(nsample, 1, d),
            jnp.stack(kp), jnp.stack(vp), jnp.stack(kip), jnp.stack(cvp), jnp.stack(plp),
            jnp.stack(ksm), jnp.stack(vsm), jnp.stack(kis), jnp.stack(cvs), jnp.stack(pls))
```
